```python
import jax, jax.numpy as jnp
from jax import lax
import numpy as np

D_MODEL = 4096
BATCH = 2
SEQ = 8192
DEPTH = 1

N_HEADS = 16
HEAD_DIM = 128
N_KV_GROUPS = 4
HEADS_PER_GROUP = N_HEADS // N_KV_GROUPS
NSA_WIDTH = N_HEADS * HEAD_DIM
KV_WIDTH = N_KV_GROUPS * HEAD_DIM
N_NSA_BRANCHES = 3
ROPE_DIM = HEAD_DIM // 4
ROPE_THETA = 500000.0
CMP_BLOCK = 32
CMP_STRIDE = 16
CMP_HIDDEN = 256
SLC_BLOCK = 64
SLC_TOPK = 16
WINDOW = 512
Q_BLOCK = 64
LRU_WIDTH = D_MODEL // 2
LRU_BLOCKS = 16
LRU_BLOCK_DIM = LRU_WIDTH // LRU_BLOCKS
CONV_WIDTH = 4
LRU_C = 8.0
N_BRANCHES = 2
FFN_HIDDEN = ((8 * D_MODEL + 2) // 3 + 255) // 256 * 256
NORM_EPS = 1e-6
NEG_INF = -1e30
POS_BIG = 1e30
IN_SIZES = (NSA_WIDTH, 6 * KV_WIDTH, N_HEADS * N_NSA_BRANCHES, LRU_WIDTH, LRU_WIDTH, N_BRANCHES * D_MODEL)
IN_WIDTH = int(sum(IN_SIZES))
IN_SPLITS = tuple(int(v) for v in np.cumsum(IN_SIZES)[:-1])

kernel_name = "hybrid_nsa_rglru_gated_block"


def _rmsnorm(x, g):
    xf = x.astype(jnp.float32)
    y = xf * lax.rsqrt(jnp.mean(xf * xf, axis=-1, keepdims=True) + NORM_EPS)
    return (y * g.astype(jnp.float32)).astype(x.dtype)


def _rope_partial(x, pos):
    half = ROPE_DIM // 2
    inv = 1.0 / (ROPE_THETA ** (jnp.arange(half, dtype=jnp.float32) * 2.0 / ROPE_DIM))
    ang = pos.astype(jnp.float32)[:, None] * inv[None, :]
    cos, sin = jnp.cos(ang), jnp.sin(ang)
    xr = x[..., :ROPE_DIM].astype(jnp.float32)
    x1, x2 = xr[..., :half], xr[..., half:]
    rot = jnp.concatenate([x1 * cos - x2 * sin, x2 * cos + x1 * sin], axis=-1)
    return jnp.concatenate([rot.astype(x.dtype), x[..., ROPE_DIM:]], axis=-1)


def _compress(kv, pos_emb, w1, b1, w2, b2):
    S = kv.shape[2]
    nc = (S - CMP_BLOCK) // CMP_STRIDE + 1
    idx = jnp.arange(nc)[:, None] * CMP_STRIDE + jnp.arange(CMP_BLOCK)[None, :]
    blocks = kv[:, :, idx, :] + pos_emb.astype(kv.dtype)
    flat = blocks.reshape(blocks.shape[:3] + (CMP_BLOCK * HEAD_DIM,))
    hid = jax.nn.gelu(flat @ w1 + b1)
    return hid @ w2 + b2


def _nsa(q, kv, gate_logits, q_norm_g, k_norm_g, cmp_pos, cmp_w1, cmp_b1, cmp_w2, cmp_b2):
    B, S, _ = q.shape
    dt = q.dtype
    f32 = jnp.float32
    pos = jnp.arange(S)
    scale = HEAD_DIM ** -0.5
    q = q.reshape(B, S, N_KV_GROUPS, HEADS_PER_GROUP, HEAD_DIM).transpose(0, 2, 3, 1, 4)
    q = _rope_partial(_rmsnorm(q, q_norm_g), pos)
    kv = kv.reshape(B, S, 6, N_KV_GROUPS, HEAD_DIM).transpose(2, 0, 3, 1, 4)
    k_cmp_raw, v_cmp_raw, k_slc, v_slc, k_win, v_win = kv[0], kv[1], kv[2], kv[3], kv[4], kv[5]

    nc = (S - CMP_BLOCK) // CMP_STRIDE + 1
    cmp_start = jnp.arange(nc) * CMP_STRIDE
    cmp_end = cmp_start + CMP_BLOCK - 1
    k_cmp = _compress(k_cmp_raw, cmp_pos[0], cmp_w1[0], cmp_b1[0], cmp_w2[0], cmp_b2[0])
    v_cmp = _compress(v_cmp_raw, cmp_pos[1], cmp_w1[1], cmp_b1[1], cmp_w2[1], cmp_b2[1])
    k_cmp = _rope_partial(_rmsnorm(k_cmp, k_norm_g[0]), cmp_end)
    k_slc = _rope_partial(_rmsnorm(k_slc, k_norm_g[1]), pos)
    k_win = _rope_partial(_rmsnorm(k_win, k_norm_g[2]), pos)

    ns = S // SLC_BLOCK
    topk = min(SLC_TOPK, ns)
    blk_ids = jnp.arange(ns)
    slc_start = blk_ids * SLC_BLOCK
    overlap = ((cmp_start[:, None] <= slc_start[None, :] + SLC_BLOCK - 1)
               & (cmp_end[:, None] >= slc_start[None, :])).astype(f32)
    k_slc_b = k_slc.reshape(B, N_KV_GROUPS, ns, SLC_BLOCK, HEAD_DIM)
    v_slc_b = v_slc.reshape(B, N_KV_GROUPS, ns, SLC_BLOCK, HEAD_DIM)
    pad = ((0, 0), (0, 0), (WINDOW, 0), (0, 0))
    k_win_p = jnp.pad(k_win, pad)
    v_win_p = jnp.pad(v_win, pad)
    gather_blocks = jax.vmap(jax.vmap(lambda kb, ix: kb[ix]))

    nq = S // Q_BLOCK
    q_blocks = q.reshape(B, N_KV_GROUPS, HEADS_PER_GROUP, nq, Q_BLOCK, HEAD_DIM).transpose(3, 0, 1, 2, 4, 5)
    g = jax.nn.sigmoid(gate_logits.astype(f32)).reshape(B, S, N_KV_GROUPS, HEADS_PER_GROUP, N_NSA_BRANCHES)
    g_blocks = g.transpose(0, 2, 3, 1, 4).reshape(B, N_KV_GROUPS, HEADS_PER_GROUP, nq, Q_BLOCK, N_NSA_BRANCHES)
    g_blocks = g_blocks.transpose(3, 0, 1, 2, 4, 5)

    def block(args):
        qb, gb, i = args
        q0 = i * Q_BLOCK
        t = q0 + jnp.arange(Q_BLOCK)
        s = jnp.einsum('bghqd,bgcd->bghqc', qb, k_cmp).astype(f32) * scale
        m = cmp_end[None, :] <= t[:, None]
        p_cmp = jax.nn.softmax(jnp.where(m, s, NEG_INF), axis=-1) * m
        o_cmp = jnp.einsum('bghqc,bgcd->bghqd', p_cmp.astype(dt), v_cmp).astype(f32)
        imp = jnp.einsum('bghqc,cn->bgqn', p_cmp, overlap)
        cur = t // SLC_BLOCK
        valid = slc_start[None, :] <= t[:, None]
        forced = ((blk_ids[None, :] == 0) | (blk_ids[None, :] == cur[:, None])
                  | (blk_ids[None, :] == cur[:, None] - 1))
        imp = jnp.where(forced, POS_BIG, jnp.where(valid, imp, NEG_INF))
        _, sel = lax.top_k(imp, topk)
        ks = gather_blocks(k_slc_b, sel)
        vs = gather_blocks(v_slc_b, sel)
        tok = sel[..., None] * SLC_BLOCK + jnp.arange(SLC_BLOCK)
        msk = tok <= t[:, None, None]
        s = jnp.einsum('bghqd,bgqnkd->bghqnk', qb, ks).astype(f32) * scale
        s = jnp.where(msk[:, :, None], s, NEG_INF).reshape(s.shape[:4] + (topk * SLC_BLOCK,))
        p = jax.nn.softmax(s, axis=-1).reshape(s.shape[:4] + (topk, SLC_BLOCK))
        o_slc = jnp.einsum('bghqnk,bgqnkd->bghqd', p.astype(dt), vs).astype(f32)
        kw = lax.dynamic_slice_in_dim(k_win_p, q0, WINDOW + Q_BLOCK, axis=2)
        vw = lax.dynamic_slice_in_dim(v_win_p, q0, WINDOW + Q_BLOCK, axis=2)
        kpos = q0 - WINDOW + jnp.arange(WINDOW + Q_BLOCK)
        mw = (kpos[None, :] >= 0) & (kpos[None, :] <= t[:, None]) & (kpos[None, :] > t[:, None] - WINDOW)
        s = jnp.einsum('bghqd,bgkd->bghqk', qb, kw).astype(f32) * scale
        p = jax.nn.softmax(jnp.where(mw, s, NEG_INF), axis=-1)
        o_win = jnp.einsum('bghqk,bgkd->bghqd', p.astype(dt), vw).astype(f32)
        o = gb[..., 0:1] * o_cmp + gb[..., 1:2] * o_slc + gb[..., 2:3] * o_win
        return o.astype(dt)

    o = lax.map(block, (q_blocks, g_blocks, jnp.arange(nq)))
    return o.transpose(1, 0, 4, 2, 3, 5).reshape(B, S, NSA_WIDTH)


def _lin_comb(left, right):
    a1, b1 = left
    a2, b2 = right
    return a1 * a2, a2 * b1 + b2


def _rglru(xb, yb, conv_w, conv_b, w_gates, b_gates, lam):
    B, S, W = xb.shape
    f32 = jnp.float32
    xc = lax.conv_general_dilated(xb, conv_w[:, None, :].astype(xb.dtype), window_strides=(1,),
                                  padding=((CONV_WIDTH - 1, 0),), dimension_numbers=('NWC', 'WIO', 'NWC'),
                                  feature_group_count=W) + conv_b
    xh = xc.reshape(B, S, LRU_BLOCKS, LRU_BLOCK_DIM)
    gates = jnp.einsum('bshc,ghcd->gbshd', xh, w_gates).reshape(2, B, S, W) + b_gates[:, None, None, :]
    r = jax.nn.sigmoid(gates[0].astype(f32))
    i = jax.nn.sigmoid(gates[1].astype(f32))
    log_a = -LRU_C * r * jax.nn.softplus(-lam.astype(f32))
    a = jnp.exp(log_a)
    b = jnp.sqrt(-jnp.expm1(2.0 * log_a)) * (i * xc.astype(f32))
    _, h = lax.associative_scan(_lin_comb, (a, b), axis=1)
    return (h * jax.nn.gelu(yb.astype(f32))).astype(xb.dtype)


def setup_inputs(seed: int = 0) -> dict:
    key = jax.random.key(seed)
    ks = jax.random.split(key, 24)
    f32 = jnp.float32
    L = DEPTH

    def nrm(k, shape, scale):
        return jax.random.normal(k, shape, f32) * scale

    u = jax.random.uniform(ks[14], (L, LRU_WIDTH), f32, 0.9, 0.999)
    a0 = u ** (1.0 / LRU_C)
    return {
        "x": nrm(ks[0], (BATCH, SEQ, D_MODEL), 1.0),
        "norm1_g": 1.0 + nrm(ks[1], (L, D_MODEL), 0.02),
        "w_in": nrm(ks[2], (L, D_MODEL, IN_WIDTH), D_MODEL ** -0.5),
        "q_norm_g": 1.0 + nrm(ks[3], (L, HEAD_DIM), 0.02),
        "k_norm_g": 1.0 + nrm(ks[4], (L, N_NSA_BRANCHES, HEAD_DIM), 0.02),
        "cmp_pos": nrm(ks[5], (L, 2, CMP_BLOCK, HEAD_DIM), 0.1),
        "cmp_w1": nrm(ks[6], (L, 2, CMP_BLOCK * HEAD_DIM, CMP_HIDDEN), (CMP_BLOCK * HEAD_DIM) ** -0.5),
        "cmp_b1": nrm(ks[7], (L, 2, CMP_HIDDEN), 0.01),
        "cmp_w2": nrm(ks[8], (L, 2, CMP_HIDDEN, HEAD_DIM), CMP_HIDDEN ** -0.5),
        "cmp_b2": nrm(ks[9], (L, 2, HEAD_DIM), 0.01),
        "conv_w": nrm(ks[10], (L, CONV_WIDTH, LRU_WIDTH), CONV_WIDTH ** -0.5),
        "conv_b": nrm(ks[11], (L, LRU_WIDTH), 0.01),
        "lru_w_gates": nrm(ks[12], (L, 2, LRU_BLOCKS, LRU_BLOCK_DIM, LRU_BLOCK_DIM), LRU_BLOCK_DIM ** -0.5),
        "lru_b_gates": nrm(ks[13], (L, 2, LRU_WIDTH), 0.01),
        "lru_lambda": jnp.log(a0) - jnp.log1p(-a0),
        "w_branch_a": nrm(ks[15], (L, NSA_WIDTH, D_MODEL), NSA_WIDTH ** -0.5),
        "w_branch_b": nrm(ks[16], (L, LRU_WIDTH, D_MODEL), LRU_WIDTH ** -0.5),
        "w_out": nrm(ks[17], (L, D_MODEL, D_MODEL), D_MODEL ** -0.5),
        "norm2_g": 1.0 + nrm(ks[18], (L, D_MODEL), 0.02),
        "w_ffn_in": nrm(ks[19], (L, D_MODEL, 2 * FFN_HIDDEN), D_MODEL ** -0.5),
        "w_ffn_out": nrm(ks[20], (L, FFN_HIDDEN, D_MODEL), FFN_HIDDEN ** -0.5),
    }


def reference(x, norm1_g, w_in, q_norm_g, k_norm_g, cmp_pos, cmp_w1, cmp_b1, cmp_w2, cmp_b2,
              conv_w, conv_b, lru_w_gates, lru_b_gates, lru_lambda, w_branch_a, w_branch_b,
              w_out, norm2_g, w_ffn_in, w_ffn_out):
    for l in range(DEPTH):
        h = _rmsnorm(x, norm1_g[l])
        proj = h @ w_in[l]
        q, kv, nsa_g, lru_x, lru_y, merge_g = jnp.split(proj, list(IN_SPLITS), axis=-1)
        y_a = _nsa(q, kv, nsa_g, q_norm_g[l], k_norm_g[l], cmp_pos[l], cmp_w1[l], cmp_b1[l],
                   cmp_w2[l], cmp_b2[l]) @ w_branch_a[l]
        y_b = _rglru(lru_x, lru_y, conv_w[l], conv_b[l], lru_w_gates[l], lru_b_gates[l],
                     lru_lambda[l]) @ w_branch_b[l]
        g = jax.nn.sigmoid(merge_g.astype(jnp.float32)).astype(x.dtype)
        mixed = g[..., :D_MODEL] * y_a + g[..., D_MODEL:] * y_b
        x = x + mixed @ w_out[l]
        h = _rmsnorm(x, norm2_g[l])
        gate, up = jnp.split(h @ w_ffn_in[l], 2, axis=-1)
        x = x + (jax.nn.silu(gate) * up) @ w_ffn_out[l]
    return x
```

```python
import functools
import math

import jax
import jax.numpy as jnp
from jax import lax
from jax.experimental import pallas as pl
from jax.experimental.pallas import tpu as pltpu

F32 = jnp.float32
_MXU_DTYPE = jnp.bfloat16

D_MODEL = 4096
N_HEADS = 16
HEAD_DIM = 128
N_GROUPS = 4
HPG = N_HEADS // N_GROUPS
NSA_W = N_HEADS * HEAD_DIM
KV_W = N_GROUPS * HEAD_DIM
ROPE_DIM = HEAD_DIM // 4
ROPE_HALF = ROPE_DIM // 2
ROPE_THETA = 500000.0
CMP_BLOCK = 32
CMP_STRIDE = 16
CMP_HIDDEN = 256
SLC_BLOCK = 64
SLC_SHIFT = 6
SLC_TOPK = 16
WINDOW = 512
LRU_W = D_MODEL // 2
LRU_BLOCKS = 16
LRU_BD = LRU_W // LRU_BLOCKS
CONV_W = 4
LRU_C = 8.0
FFN_H = 11008
NORM_EPS = 1e-6
NEG_INF = -1e30
POS_BIG = 1e30
LOG2E = 1.4426950408889634
N_GATE = 3 * HPG

LANES = 128
SUBLANES = 8
VMEM_BYTES = 64 * 1024 * 1024
VMEM_CAP = VMEM_BYTES - 6 * 1024 * 1024

SEL_LANES = 128
FFN_HP = 11264

COL_Q = 0
COL_LX = COL_Q + NSA_W
COL_LY = COL_LX + LRU_W
COL_KV = COL_LY + LRU_W
COL_MG = COL_KV + 6 * KV_W
COL_GT = COL_MG + 2 * D_MODEL
PROJ_W = COL_GT + N_GROUPS * LANES


def _nbytes(shape, dtype):
    return math.prod(shape) * jnp.dtype(dtype).itemsize


def _params(semantics, blocks, scratch=0, temps=0):
    est = 2 * sum(_nbytes(s, d) for s, d in blocks) + scratch + temps
    limit = min(int(est * 1.2) + (4 << 20), VMEM_CAP)
    return pltpu.CompilerParams(dimension_semantics=semantics, vmem_limit_bytes=limit)


def _dot(a, b):
    return jnp.dot(a, b, preferred_element_type=F32)


def _dot_nt(a, b):
    return lax.dot_general(a, b, (((1,), (1,)), ((), ())), preferred_element_type=F32)


def _rmsnorm_kernel(x_ref, g_ref, o_ref):
    x = x_ref[...]
    ms = jnp.mean(x * x, axis=-1, keepdims=True)
    o_ref[...] = ((x * lax.rsqrt(ms + NORM_EPS)) * g_ref[...]).astype(o_ref.dtype)


def _rmsnorm(x, g, tm=256):
    t, d = x.shape
    return pl.pallas_call(
        _rmsnorm_kernel,
        grid=(t // tm,),
        in_specs=[pl.BlockSpec((tm, d), lambda i: (i, 0)), pl.BlockSpec((1, d), lambda i: (0, 0))],
        out_specs=pl.BlockSpec((tm, d), lambda i: (i, 0)),
        out_shape=jax.ShapeDtypeStruct((t, d), _MXU_DTYPE),
        compiler_params=_params(("parallel",), [((tm, d), F32), ((tm, d), _MXU_DTYPE)], temps=2 * _nbytes((tm, d), F32)),
        name="rmsnorm",
    )(x, g.reshape(1, d))


def _mm_kernel(a_ref, w_ref, o_ref):
    o_ref[...] = _dot(a_ref[...], w_ref[...]).astype(o_ref.dtype)


def _in_proj(h, w, tm, tn):
    t, k = h.shape
    n = w.shape[1]
    blocks = [((tm, k), h.dtype), ((k, tn), w.dtype), ((tm, tn), F32)]
    return pl.pallas_call(
        _mm_kernel,
        grid=(t // tm, n // tn),
        in_specs=[pl.BlockSpec((tm, k), lambda i, j: (i, 0)), pl.BlockSpec((k, tn), lambda i, j: (0, j))],
        out_specs=pl.BlockSpec((tm, tn), lambda i, j: (i, j)),
        out_shape=jax.ShapeDtypeStruct((t, n), F32),
        compiler_params=_params(("parallel", "arbitrary"), blocks, temps=_nbytes((tm, tn), F32)),
        name="in_proj",
    )(h, w)


def _merge_kernel(o_ref, l_ref, wa_ref, wb_ref, ga_ref, gb_ref, out_ref):
    ya = _dot(o_ref[...], wa_ref[...])
    yb = _dot(l_ref[...], wb_ref[...])
    mixed = jax.nn.sigmoid(ga_ref[...]) * ya + jax.nn.sigmoid(gb_ref[...]) * yb
    out_ref[...] = mixed.astype(out_ref.dtype)


def _merge(o, l, wa, wb, proj, tm, tn):
    t = o.shape[0]
    ga0 = COL_MG // tn
    gb0 = (COL_MG + D_MODEL) // tn
    blocks = [((tm, NSA_W), o.dtype), ((tm, LRU_W), l.dtype), ((NSA_W, tn), wa.dtype), ((LRU_W, tn), wb.dtype),
              ((tm, tn), F32), ((tm, tn), F32), ((tm, tn), _MXU_DTYPE)]
    return pl.pallas_call(
        _merge_kernel,
        grid=(t // tm, D_MODEL // tn),
        in_specs=[pl.BlockSpec((tm, NSA_W), lambda i, j: (i, 0)),
                  pl.BlockSpec((tm, LRU_W), lambda i, j: (i, 0)),
                  pl.BlockSpec((NSA_W, tn), lambda i, j: (0, j)),
                  pl.BlockSpec((LRU_W, tn), lambda i, j: (0, j)),
                  pl.BlockSpec((tm, tn), lambda i, j: (i, ga0 + j)),
                  pl.BlockSpec((tm, tn), lambda i, j: (i, gb0 + j))],
        out_specs=pl.BlockSpec((tm, tn), lambda i, j: (i, j)),
        out_shape=jax.ShapeDtypeStruct((t, D_MODEL), _MXU_DTYPE),
        compiler_params=_params(("parallel", "arbitrary"), blocks, temps=3 * _nbytes((tm, tn), F32)),
        name="merge",
    )(o, l, wa, wb, proj, proj)


def _mm_res_kernel(a_ref, w_ref, x_ref, o_ref):
    o_ref[...] = x_ref[...] + _dot(a_ref[...], w_ref[...])


def _out_proj(a, w, x, tm, tn):
    t, k = a.shape
    n = w.shape[1]
    blocks = [((tm, k), a.dtype), ((k, tn), w.dtype), ((tm, tn), F32), ((tm, tn), F32)]
    return pl.pallas_call(
        _mm_res_kernel,
        grid=(t // tm, n // tn),
        in_specs=[pl.BlockSpec((tm, k), lambda i, j: (i, 0)), pl.BlockSpec((k, tn), lambda i, j: (0, j)),
                  pl.BlockSpec((tm, tn), lambda i, j: (i, j))],
        out_specs=pl.BlockSpec((tm, tn), lambda i, j: (i, j)),
        out_shape=jax.ShapeDtypeStruct((t, n), F32),
        compiler_params=_params(("parallel", "arbitrary"), blocks, temps=_nbytes((tm, tn), F32)),
        name="out_proj",
    )(a, w, x)


def _ffn_in_kernel(h_ref, wg_ref, wu_ref, o_ref):
    h = h_ref[...]
    gate = _dot(h, wg_ref[...])
    up = _dot(h, wu_ref[...])
    o_ref[...] = ((gate * jax.nn.sigmoid(gate)) * up).astype(o_ref.dtype)


def _ffn_in(h, wgu, tm, tn):
    t, k = h.shape
    up0 = FFN_HP // tn
    blocks = [((tm, k), h.dtype), ((k, tn), wgu.dtype), ((k, tn), wgu.dtype), ((tm, tn), _MXU_DTYPE)]
    return pl.pallas_call(
        _ffn_in_kernel,
        grid=(t // tm, FFN_HP // tn),
        in_specs=[pl.BlockSpec((tm, k), lambda i, j: (i, 0)),
                  pl.BlockSpec((k, tn), lambda i, j: (0, j)),
                  pl.BlockSpec((k, tn), lambda i, j: (0, up0 + j))],
        out_specs=pl.BlockSpec((tm, tn), lambda i, j: (i, j)),
        out_shape=jax.ShapeDtypeStruct((t, FFN_HP), _MXU_DTYPE),
        compiler_params=_params(("parallel", "arbitrary"), blocks, temps=3 * _nbytes((tm, tn), F32)),
        name="ffn_in",
    )(h, wgu, wgu)


def _ffn_out_kernel(a_ref, w_ref, x_ref, o_ref, acc_ref):
    k = pl.program_id(2)

    @pl.when(k == 0)
    def _():
        acc_ref[...] = jnp.zeros_like(acc_ref)

    acc_ref[...] += _dot(a_ref[...], w_ref[...])

    @pl.when(k == pl.num_programs(2) - 1)
    def _():
        o_ref[...] = x_ref[...] + acc_ref[...]


def _ffn_out(a, w, x, tm, tn, tk):
    t, kk = a.shape
    n = w.shape[1]
    blocks = [((tm, tk), a.dtype), ((tk, tn), w.dtype), ((tm, tn), F32), ((tm, tn), F32)]
    return pl.pallas_call(
        _ffn_out_kernel,
        grid=(t // tm, n // tn, kk // tk),
        in_specs=[pl.BlockSpec((tm, tk), lambda i, j, k: (i, k)), pl.BlockSpec((tk, tn), lambda i, j, k: (k, j)),
                  pl.BlockSpec((tm, tn), lambda i, j, k: (i, j))],
        out_specs=pl.BlockSpec((tm, tn), lambda i, j, k: (i, j)),
        out_shape=jax.ShapeDtypeStruct((t, n), F32),
        scratch_shapes=[pltpu.VMEM((tm, tn), F32)],
        compiler_params=_params(("parallel", "arbitrary", "arbitrary"), blocks,
                                scratch=_nbytes((tm, tn), F32), temps=_nbytes((tm, tn), F32)),
        name="ffn_out",
    )(a, w, x)


def _rope_tables(pos):
    inv = 1.0 / (ROPE_THETA ** (jnp.arange(ROPE_HALF, dtype=F32) * 2.0 / ROPE_DIM))
    ang = pos.astype(F32)[:, None] * inv[None, :]
    cos, sin = jnp.cos(ang), jnp.sin(ang)
    n = pos.shape[0]
    rest = HEAD_DIM - ROPE_DIM
    c = jnp.concatenate([cos, cos, jnp.ones((n, rest), F32)], axis=1)
    a = jnp.concatenate([-sin, jnp.zeros((n, HEAD_DIM - ROPE_HALF), F32)], axis=1)
    b = jnp.concatenate([jnp.zeros((n, ROPE_HALF), F32), sin, jnp.zeros((n, rest), F32)], axis=1)
    return c, a, b


def _head_norm_rope(x, g, c, a, b):
    ms = jnp.mean(x * x, axis=-1, keepdims=True)
    y = (x * lax.rsqrt(ms + NORM_EPS)) * g
    return y * c + pltpu.roll(y, HEAD_DIM - ROPE_HALF, 1) * a + pltpu.roll(y, ROPE_HALF, 1) * b


def _prep_kernel(q_ref, ks_ref, vs_ref, kw_ref, vw_ref, gq_ref, gk_ref, c_ref, a_ref, b_ref, qo_ref, kvo_ref):
    c, a, b = c_ref[...], a_ref[...], b_ref[...]
    gq = gq_ref[...]
    for h in range(N_HEADS):
        sl = slice(h * HEAD_DIM, (h + 1) * HEAD_DIM)
        qo_ref[:, sl] = _head_norm_rope(q_ref[:, sl], gq, c, a, b).astype(qo_ref.dtype)
    g_slc, g_win = gk_ref[1:2, :], gk_ref[2:3, :]
    for g in range(N_GROUPS):
        sl = slice(g * HEAD_DIM, (g + 1) * HEAD_DIM)
        kvo_ref[:, sl] = _head_norm_rope(ks_ref[:, sl], g_slc, c, a, b).astype(kvo_ref.dtype)
        kvo_ref[:, KV_W + g * HEAD_DIM:KV_W + (g + 1) * HEAD_DIM] = vs_ref[:, sl].astype(kvo_ref.dtype)
        kvo_ref[:, 2 * KV_W + g * HEAD_DIM:2 * KV_W + (g + 1) * HEAD_DIM] = (
            _head_norm_rope(kw_ref[:, sl], g_win, c, a, b).astype(kvo_ref.dtype))
        kvo_ref[:, 3 * KV_W + g * HEAD_DIM:3 * KV_W + (g + 1) * HEAD_DIM] = vw_ref[:, sl].astype(kvo_ref.dtype)


def _prep(proj, q_norm_g, k_norm_g, tables, seq, tm=256):
    t = proj.shape[0]
    kv0 = COL_KV // KV_W
    nst = seq // tm
    tab = pl.BlockSpec((tm, HEAD_DIM), lambda i: (i % nst, 0))
    blocks = [((tm, NSA_W), F32)] + [((tm, KV_W), F32)] * 4 + [((tm, HEAD_DIM), F32)] * 3 + [
        ((tm, NSA_W), _MXU_DTYPE), ((tm, 4 * KV_W), _MXU_DTYPE)]
    return pl.pallas_call(
        _prep_kernel,
        grid=(t // tm,),
        in_specs=[pl.BlockSpec((tm, NSA_W), lambda i: (i, COL_Q // NSA_W)),
                  pl.BlockSpec((tm, KV_W), lambda i: (i, kv0 + 2)),
                  pl.BlockSpec((tm, KV_W), lambda i: (i, kv0 + 3)),
                  pl.BlockSpec((tm, KV_W), lambda i: (i, kv0 + 4)),
                  pl.BlockSpec((tm, KV_W), lambda i: (i, kv0 + 5)),
                  pl.BlockSpec((1, HEAD_DIM), lambda i: (0, 0)),
                  pl.BlockSpec((3, HEAD_DIM), lambda i: (0, 0)),
                  tab, tab, tab],
        out_specs=[pl.BlockSpec((tm, NSA_W), lambda i: (i, 0)), pl.BlockSpec((tm, 4 * KV_W), lambda i: (i, 0))],
        out_shape=[jax.ShapeDtypeStruct((t, NSA_W), _MXU_DTYPE), jax.ShapeDtypeStruct((t, 4 * KV_W), _MXU_DTYPE)],
        compiler_params=_params(("parallel",), blocks, temps=8 * _nbytes((tm, HEAD_DIM), F32)),
        name="qk_prep",
    )(proj, proj, proj, proj, proj, q_norm_g.reshape(1, HEAD_DIM), k_norm_g, *tables)


def _compress_kernel(x_ref, pos_ref, w1_ref, b1_ref, w2_ref, b2_ref, g_ref, c_ref, a_ref, b_ref, o_ref, *, is_key):
    nchunk = o_ref.shape[0]
    half = CMP_BLOCK // 2
    lo = jnp.zeros((nchunk, CMP_HIDDEN), F32)
    hi = jnp.zeros((nchunk, CMP_HIDDEN), F32)
    for l in range(half):
        x = x_ref[pl.ds(l, nchunk, stride=CMP_STRIDE), :]
        w_lo = w1_ref[l * HEAD_DIM:(l + 1) * HEAD_DIM, :]
        w_hi = w1_ref[(half + l) * HEAD_DIM:(half + l + 1) * HEAD_DIM, :]
        lo = lo + _dot((x + pos_ref[l:l + 1, :]).astype(_MXU_DTYPE), w_lo)
        hi = hi + _dot((x + pos_ref[half + l:half + l + 1, :]).astype(_MXU_DTYPE), w_hi)
    hid = lo + pltpu.roll(hi, nchunk - 1, 0) + b1_ref[...]
    hid = jax.nn.gelu(hid)
    out = _dot(hid.astype(_MXU_DTYPE), w2_ref[...]) + b2_ref[...]
    if is_key:
        out = _head_norm_rope(out, g_ref[...], c_ref[...], a_ref[...], b_ref[...])
    o_ref[...] = out.astype(o_ref.dtype)


def _compress(proj3, branch, pos, w1, b1, w2, b2, g, tables, is_key):
    bsz, seq, _ = proj3.shape
    nchunk = seq // CMP_STRIDE
    col0 = COL_KV // HEAD_DIM + branch * N_GROUPS
    const = lambda shape: pl.BlockSpec(shape, lambda b, gi: (0,) * len(shape))
    blocks = [((seq, HEAD_DIM), F32), ((CMP_BLOCK * HEAD_DIM, CMP_HIDDEN), _MXU_DTYPE),
              ((nchunk, HEAD_DIM), F32), ((nchunk, HEAD_DIM), F32), ((nchunk, HEAD_DIM), F32),
              ((nchunk, HEAD_DIM), _MXU_DTYPE)]
    return pl.pallas_call(
        functools.partial(_compress_kernel, is_key=is_key),
        grid=(bsz, N_GROUPS),
        in_specs=[pl.BlockSpec((None, seq, HEAD_DIM), lambda b, gi: (b, 0, col0 + gi)),
                  const((CMP_BLOCK, HEAD_DIM)), const((CMP_BLOCK * HEAD_DIM, CMP_HIDDEN)), const((1, CMP_HIDDEN)),
                  const((CMP_HIDDEN, HEAD_DIM)), const((1, HEAD_DIM)), const((1, HEAD_DIM)),
                  const((nchunk, HEAD_DIM)), const((nchunk, HEAD_DIM)), const((nchunk, HEAD_DIM))],
        out_specs=pl.BlockSpec((None, None, nchunk, HEAD_DIM), lambda b, gi: (b, gi, 0, 0)),
        out_shape=jax.ShapeDtypeStruct((bsz, N_GROUPS, nchunk, HEAD_DIM), _MXU_DTYPE),
        compiler_params=_params(("parallel", "arbitrary"), blocks, temps=6 * _nbytes((nchunk, CMP_HIDDEN), F32)),
        name="compress_k" if is_key else "compress_v",
    )(proj3, pos, w1.astype(_MXU_DTYPE), b1.reshape(1, CMP_HIDDEN), w2.astype(_MXU_DTYPE), b2.reshape(1, HEAD_DIM),
      g.reshape(1, HEAD_DIM), *tables)


def _nsa_kernel(q_ref, kc_ref, vc_ref, ks_ref, vs_ref, kw_ref, vw_ref, gt_ref, o_ref, m_sc, l_sc, acc_sc, *, tq, tk):
    dt = q_ref.dtype
    nc = kc_ref.shape[0]
    rows = HPG * tq
    c2 = (HEAD_DIM ** -0.5) * LOG2E
    q0 = pl.program_id(2) * tq

    q = q_ref[...]
    qs = jnp.concatenate([q[:, h * HEAD_DIM:(h + 1) * HEAD_DIM] for h in range(HPG)], axis=0)
    t_q = q0 + lax.broadcasted_iota(jnp.int32, (tq, 1), 0)
    t_r = jnp.concatenate([t_q] * HPG, axis=0)

    s = _dot_nt(qs, kc_ref[...]) * c2
    cmp_end = lax.broadcasted_iota(jnp.int32, (1, nc), 1) * CMP_STRIDE + (CMP_BLOCK - 1)
    m = cmp_end <= t_r
    sm = jnp.where(m, s, NEG_INF)
    e = jnp.where(m, jnp.exp2(sm - jnp.max(sm, axis=1, keepdims=True)), 0.0)
    den = jnp.sum(e, axis=1, keepdims=True)
    p = e * (1.0 / jnp.where(den > 0.0, den, 1.0))
    o_cmp = _dot(p.astype(dt), vc_ref[...])

    psum = p[0:tq]
    for h in range(1, HPG):
        psum = psum + p[h * tq:(h + 1) * tq]
    blk_c = lax.broadcasted_iota(jnp.int32, (SEL_LANES, nc), 0)
    cmp_c = lax.broadcasted_iota(jnp.int32, (SEL_LANES, nc), 1)
    ratio = SLC_BLOCK // CMP_STRIDE
    back = (CMP_BLOCK - 1) // CMP_STRIDE
    overlap_t = ((cmp_c >= ratio * blk_c - back) & (cmp_c <= ratio * blk_c + (ratio - 1))).astype(dt)
    p_hi = psum.astype(dt)
    p_lo = (psum - p_hi.astype(F32)).astype(dt)
    imp = _dot_nt(overlap_t, p_hi) + _dot_nt(overlap_t, p_lo)

    blk = lax.broadcasted_iota(jnp.int32, (SEL_LANES, tq), 0)
    t_l = q0 + lax.broadcasted_iota(jnp.int32, (SEL_LANES, tq), 1)
    cur = jnp.right_shift(t_l, SLC_SHIFT)
    forced = (blk == 0) | (blk == cur) | (blk == cur - 1)
    valid = blk * SLC_BLOCK <= t_l
    work = jnp.where(forced, POS_BIG, jnp.where(valid, imp, NEG_INF))
    blk_f = blk.astype(F32)
    sel = jnp.zeros((SEL_LANES, tq), jnp.bool_)
    for _ in range(SLC_TOPK):
        top = jnp.max(work, axis=0, keepdims=True)
        first = jnp.min(jnp.where(work == top, blk_f, float(SEL_LANES)), axis=0, keepdims=True)
        pick = blk_f == first
        sel = sel | pick
        work = jnp.where(pick, -jnp.inf, work)
    bias = jnp.where(sel, 0.0, NEG_INF).T.astype(dt)

    qa = jnp.concatenate([qs, jnp.concatenate([bias] * HPG, axis=0)], axis=1)
    m_sc[...] = jnp.full_like(m_sc, NEG_INF)
    l_sc[...] = jnp.zeros_like(l_sc)
    acc_sc[...] = jnp.zeros_like(acc_sc)

    def slc_tile(k0, causal):
        k = ks_ref[pl.ds(k0, tk), :]
        kblk = jnp.right_shift(k0 + lax.broadcasted_iota(jnp.int32, (tk, SEL_LANES), 0), SLC_SHIFT)
        onehot = (kblk == lax.broadcasted_iota(jnp.int32, (tk, SEL_LANES), 1)).astype(dt)
        sc = _dot_nt(qa, jnp.concatenate([k, onehot], axis=1)) * c2
        if causal:
            kpos = k0 + lax.broadcasted_iota(jnp.int32, (1, tk), 1)
            sc = jnp.where(kpos <= t_r, sc, NEG_INF)
        m_old = m_sc[...]
        m_new = jnp.maximum(m_old, jnp.max(sc, axis=1, keepdims=True))
        alpha = jnp.exp2(m_old - m_new)
        pr = jnp.exp2(sc - m_new)
        l_sc[...] = alpha * l_sc[...] + jnp.sum(pr, axis=1, keepdims=True)
        acc_sc[...] = alpha * acc_sc[...] + _dot(pr.astype(dt), vs_ref[pl.ds(k0, tk), :])
        m_sc[...] = m_new

    n_full = q0 // tk

    def body(kt, carry):
        slc_tile(pl.multiple_of(kt * tk, tk), False)
        return carry

    lax.fori_loop(0, n_full, body, 0)
    slc_tile(pl.multiple_of(n_full * tk, tk), True)
    o_slc = acc_sc[...] * (1.0 / l_sc[...])

    wlen = WINDOW + tq
    k0w = pl.multiple_of(jnp.maximum(q0 - WINDOW, 0), tq)
    sw = _dot_nt(qs, kw_ref[pl.ds(k0w, wlen), :]) * c2
    kpos = k0w + lax.broadcasted_iota(jnp.int32, (1, wlen), 1)
    mw = (kpos <= t_r) & (kpos > t_r - WINDOW)
    sw = jnp.where(mw, sw, NEG_INF)
    ew = jnp.exp2(sw - jnp.max(sw, axis=1, keepdims=True))
    o_win = _dot(ew.astype(dt), vw_ref[pl.ds(k0w, wlen), :]) * (1.0 / jnp.sum(ew, axis=1, keepdims=True))

    gate = jax.nn.sigmoid(gt_ref[...])
    for h in range(HPG):
        r = slice(h * tq, (h + 1) * tq)
        o = (gate[:, 3 * h:3 * h + 1] * o_cmp[r] + gate[:, 3 * h + 1:3 * h + 2] * o_slc[r]
             + gate[:, 3 * h + 2:3 * h + 3] * o_win[r])
        o_ref[:, h * HEAD_DIM:(h + 1) * HEAD_DIM] = o.astype(o_ref.dtype)


def _nsa(qh, kvh, k_cmp, v_cmp, proj3, tq=128, tk=512):
    bsz, seq, _ = qh.shape
    nc = k_cmp.shape[2]
    assert seq % tk == 0 and tk % tq == 0 and seq >= WINDOW + tq
    assert seq // SLC_BLOCK <= SEL_LANES and seq // SLC_BLOCK >= SLC_TOPK
    gt0 = COL_GT // LANES
    rows = HPG * tq
    qw = HPG * HEAD_DIM
    kvspec = lambda off: pl.BlockSpec((None, seq, HEAD_DIM), lambda b, g, i: (b, 0, off + g))
    cspec = pl.BlockSpec((None, None, nc, HEAD_DIM), lambda b, g, i: (b, g, 0, 0))
    blocks = [((tq, qw), qh.dtype)] * 2 + [((nc, HEAD_DIM), qh.dtype)] * 2 + [((seq, HEAD_DIM), qh.dtype)] * 4 + [
        ((tq, LANES), F32)]
    scratch = 2 * _nbytes((rows, LANES), F32) + _nbytes((rows, HEAD_DIM), F32)
    return pl.pallas_call(
        functools.partial(_nsa_kernel, tq=tq, tk=tk),
        grid=(bsz, N_GROUPS, seq // tq),
        in_specs=[pl.BlockSpec((None, tq, qw), lambda b, g, i: (b, i, g)),
                  cspec, cspec,
                  kvspec(0), kvspec(N_GROUPS), kvspec(2 * N_GROUPS), kvspec(3 * N_GROUPS),
                  pl.BlockSpec((None, tq, LANES), lambda b, g, i: (b, i, gt0 + g))],
        out_specs=pl.BlockSpec((None, tq, qw), lambda b, g, i: (b, i, g)),
        out_shape=jax.ShapeDtypeStruct((bsz, seq, NSA_W), _MXU_DTYPE),
        scratch_shapes=[pltpu.VMEM((rows, 1), F32), pltpu.VMEM((rows, 1), F32), pltpu.VMEM((rows, HEAD_DIM), F32)],
        compiler_params=_params(("parallel", "parallel", "arbitrary"), blocks, scratch=scratch,
                                temps=8 * _nbytes((rows, WINDOW + tq), F32)),
        name="nsa_attention",
    )(qh, k_cmp, v_cmp, kvh, kvh, kvh, kvh, proj3)


def _lru_kernel(x_ref, y_ref, cw_ref, cb_ref, wg_ref, bg_ref, lam_ref, o_ref, xbuf, a_sc, b_sc, h_sc, hprev, *, ts):
    @pl.when(pl.program_id(1) == 0)
    def _():
        xbuf[0:SUBLANES, :] = jnp.zeros((SUBLANES, LRU_W), F32)
        hprev[...] = jnp.zeros_like(hprev)

    x = x_ref[...]
    xbuf[SUBLANES:SUBLANES + ts, :] = x
    xc = cb_ref[...] + cw_ref[CONV_W - 1:CONV_W, :] * x
    for d in range(1, CONV_W):
        xc = xc + cw_ref[CONV_W - 1 - d:CONV_W - d, :] * xbuf[SUBLANES - d:SUBLANES - d + ts, :]
    xbuf[0:SUBLANES, :] = x[ts - SUBLANES:ts, :]

    z = -lam_ref[...]
    softplus = jnp.maximum(z, 0.0) + jnp.log1p(jnp.exp(-jnp.abs(z)))
    for k in range(LRU_BLOCKS):
        sl = slice(k * LRU_BD, (k + 1) * LRU_BD)
        xk = xc[:, sl]
        gates = _dot(xk.astype(_MXU_DTYPE), wg_ref[k])
        r = jax.nn.sigmoid(gates[:, :LRU_BD] + bg_ref[0:1, sl])
        i = jax.nn.sigmoid(gates[:, LRU_BD:] + bg_ref[1:2, sl])
        log_a = (-LRU_C * r) * softplus[:, sl]
        a = jnp.exp(log_a)
        a_sc[:, sl] = a
        b_sc[:, sl] = jnp.sqrt(-jnp.tanh(log_a) * (a * a + 1.0)) * (i * xk)

    row = lax.broadcasted_iota(jnp.int32, (SUBLANES, LRU_W), 0)

    def chunk(c, h_in):
        r0 = pl.multiple_of(c * SUBLANES, SUBLANES)
        a = a_sc[pl.ds(r0, SUBLANES), :]
        b = b_sc[pl.ds(r0, SUBLANES), :]
        d = 1
        while d < SUBLANES:
            keep = row >= d
            a_prev = jnp.where(keep, pltpu.roll(a, d, 0), 1.0)
            b_prev = jnp.where(keep, pltpu.roll(b, d, 0), 0.0)
            b = a * b_prev + b
            a = a * a_prev
            d *= 2
        h = a * h_in + b
        h_sc[pl.ds(r0, SUBLANES), :] = h
        return jnp.broadcast_to(h[SUBLANES - 1:SUBLANES, :], (SUBLANES, LRU_W))

    hprev[...] = lax.fori_loop(0, ts // SUBLANES, chunk, hprev[...])
    o_ref[...] = (h_sc[...] * jax.nn.gelu(y_ref[...])).astype(o_ref.dtype)


def _lru(proj3, conv_w, conv_b, wg, bg, lam, ts=256):
    bsz, seq, _ = proj3.shape
    const = lambda shape: pl.BlockSpec(shape, lambda b, s: (0,) * len(shape))
    blocks = [((ts, LRU_W), F32)] * 2 + [((LRU_BLOCKS, LRU_BD, 2 * LRU_BD), _MXU_DTYPE), ((ts, LRU_W), _MXU_DTYPE)]
    scratch = 3 * _nbytes((ts, LRU_W), F32) + _nbytes((ts + 2 * SUBLANES, LRU_W), F32)
    return pl.pallas_call(
        functools.partial(_lru_kernel, ts=ts),
        grid=(bsz, seq // ts),
        in_specs=[pl.BlockSpec((None, ts, LRU_W), lambda b, s: (b, s, COL_LX // LRU_W)),
                  pl.BlockSpec((None, ts, LRU_W), lambda b, s: (b, s, COL_LY // LRU_W)),
                  const((CONV_W, LRU_W)), const((1, LRU_W)), const((LRU_BLOCKS, LRU_BD, 2 * LRU_BD)),
                  const((2, LRU_W)), const((1, LRU_W))],
        out_specs=pl.BlockSpec((None, ts, LRU_W), lambda b, s: (b, s, 0)),
        out_shape=jax.ShapeDtypeStruct((bsz, seq, LRU_W), _MXU_DTYPE),
        scratch_shapes=[pltpu.VMEM((ts + SUBLANES, LRU_W), F32), pltpu.VMEM((ts, LRU_W), F32),
                        pltpu.VMEM((ts, LRU_W), F32), pltpu.VMEM((ts, LRU_W), F32),
                        pltpu.VMEM((SUBLANES, LRU_W), F32)],
        compiler_params=_params(("parallel", "arbitrary"), blocks, scratch=scratch,
                                temps=4 * _nbytes((ts, LRU_W), F32)),
        name="rglru",
    )(proj3, proj3, conv_w, conv_b.reshape(1, LRU_W), wg, bg, lam.reshape(1, LRU_W))


def _pack_w_in(w_in):
    o_q, o_kv, o_gt = 0, NSA_W, NSA_W + 6 * KV_W
    o_lx = o_gt + N_HEADS * 3
    o_ly = o_lx + LRU_W
    o_mg = o_ly + LRU_W
    gt = w_in[:, o_gt:o_lx].reshape(D_MODEL, N_GROUPS, N_GATE)
    gt = jnp.pad(gt, ((0, 0), (0, 0), (0, LANES - N_GATE))).reshape(D_MODEL, N_GROUPS * LANES)
    parts = [w_in[:, o_q:o_kv], w_in[:, o_lx:o_ly], w_in[:, o_ly:o_mg], w_in[:, o_kv:o_gt], w_in[:, o_mg:], gt]
    return jnp.concatenate([p.astype(_MXU_DTYPE) for p in parts], axis=1)


def _layer(x, norm1_g, w_in, q_norm_g, k_norm_g, cmp_pos, cmp_w1, cmp_b1, cmp_w2, cmp_b2, conv_w, conv_b,
           lru_w_gates, lru_b_gates, lru_lambda, w_branch_a, w_branch_b, w_out, norm2_g, w_ffn_in, w_ffn_out):
    bsz, seq, _ = x.shape
    t = bsz * seq
    tm = min(1024, t)
    x2 = x.reshape(t, D_MODEL)

    h = _rmsnorm(x2, norm1_g)
    proj = _in_proj(h, _pack_w_in(w_in), tm, 512)
    proj3 = proj.reshape(bsz, seq, PROJ_W)

    tables = _rope_tables(jnp.arange(seq))
    qh, kvh = _prep(proj, q_norm_g, k_norm_g, tables, seq)
    cmp_tables = _rope_tables(jnp.arange(seq // CMP_STRIDE) * CMP_STRIDE + (CMP_BLOCK - 1))
    k_cmp = _compress(proj3, 0, cmp_pos[0], cmp_w1[0], cmp_b1[0], cmp_w2[0], cmp_b2[0], k_norm_g[0], cmp_tables, True)
    v_cmp = _compress(proj3, 1, cmp_pos[1], cmp_w1[1], cmp_b1[1], cmp_w2[1], cmp_b2[1], k_norm_g[0], cmp_tables, False)
    o_nsa = _nsa(qh.reshape(bsz, seq, NSA_W), kvh.reshape(bsz, seq, 4 * KV_W), k_cmp, v_cmp, proj3)

    wg = jnp.concatenate([lru_w_gates[0], lru_w_gates[1]], axis=-1).astype(_MXU_DTYPE)
    o_lru = _lru(proj3, conv_w, conv_b, wg, lru_b_gates, lru_lambda)

    mixed = _merge(o_nsa.reshape(t, NSA_W), o_lru.reshape(t, LRU_W), w_branch_a.astype(_MXU_DTYPE),
                   w_branch_b.astype(_MXU_DTYPE), proj, tm, 512)
    x1 = _out_proj(mixed, w_out.astype(_MXU_DTYPE), x2, tm, 512)

    h2 = _rmsnorm(x1, norm2_g)
    pad = FFN_HP - FFN_H
    wgu = jnp.concatenate([jnp.pad(w_ffn_in[:, :FFN_H].astype(_MXU_DTYPE), ((0, 0), (0, pad))),
                           jnp.pad(w_ffn_in[:, FFN_H:].astype(_MXU_DTYPE), ((0, 0), (0, pad)))], axis=1)
    act = _ffn_in(h2, wgu, tm, 512)
    w2 = jnp.pad(w_ffn_out.astype(_MXU_DTYPE), ((0, pad), (0, 0)))
    out = _ffn_out(act, w2, x1, tm, 512, FFN_HP // 4)
    return out.reshape(bsz, seq, D_MODEL)


def kernel(x, norm1_g, w_in, q_norm_g, k_norm_g, cmp_pos, cmp_w1, cmp_b1, cmp_w2, cmp_b2, conv_w, conv_b,
           lru_w_gates, lru_b_gates, lru_lambda, w_branch_a, w_branch_b, w_out, norm2_g, w_ffn_in, w_ffn_out):
    for l in range(norm1_g.shape[0]):
        x = _layer(x, norm1_g[l], w_in[l], q_norm_g[l], k_norm_g[l], cmp_pos[l], cmp_w1[l], cmp_b1[l], cmp_w2[l],
                   cmp_b2[l], conv_w[l], conv_b[l], lru_w_gates[l], lru_b_gates[l], lru_lambda[l], w_branch_a[l],
                   w_branch_b[l], w_out[l], norm2_g[l], w_ffn_in[l], w_ffn_out[l])
    return x
```

```python
import functools
import math

import jax
import jax.numpy as jnp
from jax import lax
from jax.experimental import pallas as pl
from jax.experimental.pallas import tpu as pltpu

F32 = jnp.float32
_MXU_DTYPE = jnp.bfloat16

D_MODEL = 4096
N_HEADS = 16
HEAD_DIM = 128
N_GROUPS = 4
HPG = N_HEADS // N_GROUPS
NSA_W = N_HEADS * HEAD_DIM
KV_W = N_GROUPS * HEAD_DIM
ROPE_DIM = HEAD_DIM // 4
ROPE_HALF = ROPE_DIM // 2
ROPE_THETA = 500000.0
CMP_BLOCK = 32
CMP_STRIDE = 16
CMP_HIDDEN = 256
SLC_BLOCK = 64
SLC_SHIFT = 6
SLC_TOPK = 16
WINDOW = 512
LRU_W = D_MODEL // 2
LRU_BLOCKS = 16
LRU_BD = LRU_W // LRU_BLOCKS
CONV_W = 4
LRU_C = 8.0
FFN_H = 11008
NORM_EPS = 1e-6
NEG_INF = -1e30
POS_BIG = 1e30
LOG2E = 1.4426950408889634
N_GATE = 3 * HPG

LANES = 128
SUBLANES = 8
VMEM_BYTES = 64 * 1024 * 1024
VMEM_CAP = VMEM_BYTES - 6 * 1024 * 1024

SEL_BLOCKS = 128
KEY_TILE = 128
SLC_PARTS = 1
SCORE_SCALE = (HEAD_DIM ** -0.5) * LOG2E
FFN_HP = 11264

COL_Q = 0
COL_LX = COL_Q + NSA_W
COL_LY = COL_LX + LRU_W
COL_KV = COL_LY + LRU_W
COL_MG = COL_KV + 6 * KV_W
COL_GT = COL_MG + 2 * D_MODEL
PROJ_W = COL_GT + N_GROUPS * LANES


def _nbytes(shape, dtype):
    return math.prod(shape) * jnp.dtype(dtype).itemsize


def _params(semantics, blocks, scratch=0, temps=0):
    est = 2 * sum(_nbytes(s, d) for s, d in blocks) + scratch + temps
    limit = min(int(est * 1.2) + (4 << 20), VMEM_CAP)
    return pltpu.CompilerParams(dimension_semantics=semantics, vmem_limit_bytes=limit)


def _dot(a, b):
    return jnp.dot(a, b, preferred_element_type=F32)


def _rmsnorm_kernel(x_ref, g_ref, o_ref):
    x = x_ref[...]
    ms = jnp.mean(x * x, axis=-1, keepdims=True)
    o_ref[...] = ((x * lax.rsqrt(ms + NORM_EPS)) * g_ref[...]).astype(o_ref.dtype)


def _rmsnorm(x, g, tm=256):
    t, d = x.shape
    return pl.pallas_call(
        _rmsnorm_kernel,
        grid=(t // tm,),
        in_specs=[pl.BlockSpec((tm, d), lambda i: (i, 0)), pl.BlockSpec((1, d), lambda i: (0, 0))],
        out_specs=pl.BlockSpec((tm, d), lambda i: (i, 0)),
        out_shape=jax.ShapeDtypeStruct((t, d), _MXU_DTYPE),
        compiler_params=_params(("parallel",), [((tm, d), F32), ((tm, d), _MXU_DTYPE)], temps=2 * _nbytes((tm, d), F32)),
        name="rmsnorm",
    )(x, g.reshape(1, d))


def _mm_kernel(a_ref, w_ref, o_ref):
    o_ref[...] = _dot(a_ref[...], w_ref[...]).astype(o_ref.dtype)


def _in_proj(h, w, tm, tn):
    t, k = h.shape
    n = w.shape[1]
    blocks = [((tm, k), h.dtype), ((k, tn), w.dtype), ((tm, tn), F32)]
    return pl.pallas_call(
        _mm_kernel,
        grid=(t // tm, n // tn),
        in_specs=[pl.BlockSpec((tm, k), lambda i, j: (i, 0)), pl.BlockSpec((k, tn), lambda i, j: (0, j))],
        out_specs=pl.BlockSpec((tm, tn), lambda i, j: (i, j)),
        out_shape=jax.ShapeDtypeStruct((t, n), F32),
        compiler_params=_params(("parallel", "arbitrary"), blocks, temps=_nbytes((tm, tn), F32)),
        name="in_proj",
    )(h, w)


def _merge_kernel(o_ref, l_ref, wa_ref, wb_ref, ga_ref, gb_ref, out_ref):
    ya = _dot(o_ref[...], wa_ref[...])
    yb = _dot(l_ref[...], wb_ref[...])
    mixed = jax.nn.sigmoid(ga_ref[...]) * ya + jax.nn.sigmoid(gb_ref[...]) * yb
    out_ref[...] = mixed.astype(out_ref.dtype)


def _merge(o, l, wa, wb, proj, tm, tn):
    t = o.shape[0]
    ga0 = COL_MG // tn
    gb0 = (COL_MG + D_MODEL) // tn
    blocks = [((tm, NSA_W), o.dtype), ((tm, LRU_W), l.dtype), ((NSA_W, tn), wa.dtype), ((LRU_W, tn), wb.dtype),
              ((tm, tn), F32), ((tm, tn), F32), ((tm, tn), _MXU_DTYPE)]
    return pl.pallas_call(
        _merge_kernel,
        grid=(t // tm, D_MODEL // tn),
        in_specs=[pl.BlockSpec((tm, NSA_W), lambda i, j: (i, 0)),
                  pl.BlockSpec((tm, LRU_W), lambda i, j: (i, 0)),
                  pl.BlockSpec((NSA_W, tn), lambda i, j: (0, j)),
                  pl.BlockSpec((LRU_W, tn), lambda i, j: (0, j)),
                  pl.BlockSpec((tm, tn), lambda i, j: (i, ga0 + j)),
                  pl.BlockSpec((tm, tn), lambda i, j: (i, gb0 + j))],
        out_specs=pl.BlockSpec((tm, tn), lambda i, j: (i, j)),
        out_shape=jax.ShapeDtypeStruct((t, D_MODEL), _MXU_DTYPE),
        compiler_params=_params(("parallel", "arbitrary"), blocks, temps=3 * _nbytes((tm, tn), F32)),
        name="merge",
    )(o, l, wa, wb, proj, proj)


def _mm_res_kernel(a_ref, w_ref, x_ref, o_ref):
    o_ref[...] = x_ref[...] + _dot(a_ref[...], w_ref[...])


def _out_proj(a, w, x, tm, tn):
    t, k = a.shape
    n = w.shape[1]
    blocks = [((tm, k), a.dtype), ((k, tn), w.dtype), ((tm, tn), F32), ((tm, tn), F32)]
    return pl.pallas_call(
        _mm_res_kernel,
        grid=(t // tm, n // tn),
        in_specs=[pl.BlockSpec((tm, k), lambda i, j: (i, 0)), pl.BlockSpec((k, tn), lambda i, j: (0, j)),
                  pl.BlockSpec((tm, tn), lambda i, j: (i, j))],
        out_specs=pl.BlockSpec((tm, tn), lambda i, j: (i, j)),
        out_shape=jax.ShapeDtypeStruct((t, n), F32),
        compiler_params=_params(("parallel", "arbitrary"), blocks, temps=_nbytes((tm, tn), F32)),
        name="out_proj",
    )(a, w, x)


def _ffn_in_kernel(h_ref, wg_ref, wu_ref, o_ref):
    h = h_ref[...]
    gate = _dot(h, wg_ref[...])
    up = _dot(h, wu_ref[...])
    o_ref[...] = ((gate * jax.nn.sigmoid(gate)) * up).astype(o_ref.dtype)


def _ffn_in(h, wgu, tm, tn):
    t, k = h.shape
    up0 = FFN_HP // tn
    blocks = [((tm, k), h.dtype), ((k, tn), wgu.dtype), ((k, tn), wgu.dtype), ((tm, tn), _MXU_DTYPE)]
    return pl.pallas_call(
        _ffn_in_kernel,
        grid=(t // tm, FFN_HP // tn),
        in_specs=[pl.BlockSpec((tm, k), lambda i, j: (i, 0)),
                  pl.BlockSpec((k, tn), lambda i, j: (0, j)),
                  pl.BlockSpec((k, tn), lambda i, j: (0, up0 + j))],
        out_specs=pl.BlockSpec((tm, tn), lambda i, j: (i, j)),
        out_shape=jax.ShapeDtypeStruct((t, FFN_HP), _MXU_DTYPE),
        compiler_params=_params(("parallel", "arbitrary"), blocks, temps=3 * _nbytes((tm, tn), F32)),
        name="ffn_in",
    )(h, wgu, wgu)


def _ffn_out_kernel(a_ref, w_ref, x_ref, o_ref, acc_ref):
    k = pl.program_id(2)

    @pl.when(k == 0)
    def _():
        acc_ref[...] = jnp.zeros_like(acc_ref)

    acc_ref[...] += _dot(a_ref[...], w_ref[...])

    @pl.when(k == pl.num_programs(2) - 1)
    def _():
        o_ref[...] = x_ref[...] + acc_ref[...]


def _ffn_out(a, w, x, tm, tn, tk):
    t, kk = a.shape
    n = w.shape[1]
    blocks = [((tm, tk), a.dtype), ((tk, tn), w.dtype), ((tm, tn), F32), ((tm, tn), F32)]
    return pl.pallas_call(
        _ffn_out_kernel,
        grid=(t // tm, n // tn, kk // tk),
        in_specs=[pl.BlockSpec((tm, tk), lambda i, j, k: (i, k)), pl.BlockSpec((tk, tn), lambda i, j, k: (k, j)),
                  pl.BlockSpec((tm, tn), lambda i, j, k: (i, j))],
        out_specs=pl.BlockSpec((tm, tn), lambda i, j, k: (i, j)),
        out_shape=jax.ShapeDtypeStruct((t, n), F32),
        scratch_shapes=[pltpu.VMEM((tm, tn), F32)],
        compiler_params=_params(("parallel", "arbitrary", "arbitrary"), blocks,
                                scratch=_nbytes((tm, tn), F32), temps=_nbytes((tm, tn), F32)),
        name="ffn_out",
    )(a, w, x)


def _rope_tables(pos):
    inv = 1.0 / (ROPE_THETA ** (jnp.arange(ROPE_HALF, dtype=F32) * 2.0 / ROPE_DIM))
    ang = pos.astype(F32)[:, None] * inv[None, :]
    cos, sin = jnp.cos(ang), jnp.sin(ang)
    n = pos.shape[0]
    rest = HEAD_DIM - ROPE_DIM
    c = jnp.concatenate([cos, cos, jnp.ones((n, rest), F32)], axis=1)
    a = jnp.concatenate([-sin, jnp.zeros((n, HEAD_DIM - ROPE_HALF), F32)], axis=1)
    b = jnp.concatenate([jnp.zeros((n, ROPE_HALF), F32), sin, jnp.zeros((n, rest), F32)], axis=1)
    return c, a, b


def _head_norm_rope(x, g, c, a, b):
    ms = jnp.mean(x * x, axis=-1, keepdims=True)
    y = (x * lax.rsqrt(ms + NORM_EPS)) * g
    return y * c + pltpu.roll(y, HEAD_DIM - ROPE_HALF, 1) * a + pltpu.roll(y, ROPE_HALF, 1) * b


def _prep_kernel(q_ref, ks_ref, vs_ref, kw_ref, vw_ref, gq_ref, gk_ref, c_ref, a_ref, b_ref,
                 qo_ref, ko_ref, vst_ref, vwt_ref):
    c, a, b = c_ref[...], a_ref[...], b_ref[...]
    gq = gq_ref[...]
    for h in range(N_HEADS):
        sl = slice(h * HEAD_DIM, (h + 1) * HEAD_DIM)
        qo_ref[:, sl] = (_head_norm_rope(q_ref[:, sl], gq, c, a, b) * SCORE_SCALE).astype(qo_ref.dtype)
    g_slc, g_win = gk_ref[1:2, :], gk_ref[2:3, :]
    for g in range(N_GROUPS):
        sl = slice(g * HEAD_DIM, (g + 1) * HEAD_DIM)
        ko_ref[:, sl] = _head_norm_rope(ks_ref[:, sl], g_slc, c, a, b).astype(ko_ref.dtype)
        ko_ref[:, KV_W + g * HEAD_DIM:KV_W + (g + 1) * HEAD_DIM] = (
            _head_norm_rope(kw_ref[:, sl], g_win, c, a, b).astype(ko_ref.dtype))
        for j in range(vst_ref.shape[1]):
            r = slice(j * KEY_TILE, (j + 1) * KEY_TILE)
            vst_ref[g, j] = vs_ref[r, sl].T.astype(vst_ref.dtype)
            vwt_ref[g, j] = vw_ref[r, sl].T.astype(vwt_ref.dtype)


def _prep(proj, q_norm_g, k_norm_g, tables, bsz, seq, tm=256):
    t = proj.shape[0]
    kv0 = COL_KV // KV_W
    nst = seq // tm
    tab = pl.BlockSpec((tm, HEAD_DIM), lambda i: (i % nst, 0))
    vt_spec = pl.BlockSpec((None, N_GROUPS, tm // KEY_TILE, HEAD_DIM, KEY_TILE),
                           lambda i: (i // nst, 0, i % nst, 0, 0))
    vt_shape = jax.ShapeDtypeStruct((bsz, N_GROUPS, seq // KEY_TILE, HEAD_DIM, KEY_TILE), _MXU_DTYPE)
    blocks = [((tm, NSA_W), F32)] + [((tm, KV_W), F32)] * 4 + [((tm, HEAD_DIM), F32)] * 3 + [
        ((tm, NSA_W), _MXU_DTYPE), ((tm, 4 * KV_W), _MXU_DTYPE)]
    return pl.pallas_call(
        _prep_kernel,
        grid=(t // tm,),
        in_specs=[pl.BlockSpec((tm, NSA_W), lambda i: (i, COL_Q // NSA_W)),
                  pl.BlockSpec((tm, KV_W), lambda i: (i, kv0 + 2)),
                  pl.BlockSpec((tm, KV_W), lambda i: (i, kv0 + 3)),
                  pl.BlockSpec((tm, KV_W), lambda i: (i, kv0 + 4)),
                  pl.BlockSpec((tm, KV_W), lambda i: (i, kv0 + 5)),
                  pl.BlockSpec((1, HEAD_DIM), lambda i: (0, 0)),
                  pl.BlockSpec((3, HEAD_DIM), lambda i: (0, 0)),
                  tab, tab, tab],
        out_specs=[pl.BlockSpec((tm, NSA_W), lambda i: (i, 0)), pl.BlockSpec((tm, 2 * KV_W), lambda i: (i, 0)),
                   vt_spec, vt_spec],
        out_shape=[jax.ShapeDtypeStruct((t, NSA_W), _MXU_DTYPE), jax.ShapeDtypeStruct((t, 2 * KV_W), _MXU_DTYPE),
                   vt_shape, vt_shape],
        compiler_params=_params(("parallel",), blocks, temps=8 * _nbytes((tm, HEAD_DIM), F32)),
        name="qk_prep",
    )(proj, proj, proj, proj, proj, q_norm_g.reshape(1, HEAD_DIM), k_norm_g, *tables)


def _compress_kernel(x_ref, pos_ref, w1_ref, b1_ref, w2_ref, b2_ref, g_ref, c_ref, a_ref, b_ref, o_ref, *, is_key):
    nchunk = x_ref.shape[0] // CMP_STRIDE
    half = CMP_BLOCK // 2
    lo = jnp.zeros((nchunk, CMP_HIDDEN), F32)
    hi = jnp.zeros((nchunk, CMP_HIDDEN), F32)
    for l in range(half):
        x = x_ref[pl.ds(l, nchunk, stride=CMP_STRIDE), :]
        w_lo = w1_ref[l * HEAD_DIM:(l + 1) * HEAD_DIM, :]
        w_hi = w1_ref[(half + l) * HEAD_DIM:(half + l + 1) * HEAD_DIM, :]
        lo = lo + _dot((x + pos_ref[l:l + 1, :]).astype(_MXU_DTYPE), w_lo)
        hi = hi + _dot((x + pos_ref[half + l:half + l + 1, :]).astype(_MXU_DTYPE), w_hi)
    hid = lo + pltpu.roll(hi, nchunk - 1, 0) + b1_ref[...]
    hid = jax.nn.gelu(hid)
    out = _dot(hid.astype(_MXU_DTYPE), w2_ref[...]) + b2_ref[...]
    if is_key:
        o_ref[...] = _head_norm_rope(out, g_ref[...], c_ref[...], a_ref[...], b_ref[...]).astype(o_ref.dtype)
    else:
        o_ref[...] = out.T.astype(o_ref.dtype)


def _compress(proj3, branch, pos, w1, b1, w2, b2, g, tables, is_key):
    bsz, seq, _ = proj3.shape
    nchunk = seq // CMP_STRIDE
    col0 = COL_KV // HEAD_DIM + branch * N_GROUPS
    oshape = (nchunk, HEAD_DIM) if is_key else (HEAD_DIM, nchunk)
    const = lambda shape: pl.BlockSpec(shape, lambda b, gi: (0,) * len(shape))
    blocks = [((seq, HEAD_DIM), F32), ((CMP_BLOCK * HEAD_DIM, CMP_HIDDEN), _MXU_DTYPE),
              ((nchunk, HEAD_DIM), F32), ((nchunk, HEAD_DIM), F32), ((nchunk, HEAD_DIM), F32),
              ((nchunk, HEAD_DIM), _MXU_DTYPE)]
    return pl.pallas_call(
        functools.partial(_compress_kernel, is_key=is_key),
        grid=(bsz, N_GROUPS),
        in_specs=[pl.BlockSpec((None, seq, HEAD_DIM), lambda b, gi: (b, 0, col0 + gi)),
                  const((CMP_BLOCK, HEAD_DIM)), const((CMP_BLOCK * HEAD_DIM, CMP_HIDDEN)), const((1, CMP_HIDDEN)),
                  const((CMP_HIDDEN, HEAD_DIM)), const((1, HEAD_DIM)), const((1, HEAD_DIM)),
                  const((nchunk, HEAD_DIM)), const((nchunk, HEAD_DIM)), const((nchunk, HEAD_DIM))],
        out_specs=pl.BlockSpec((None, None) + oshape, lambda b, gi: (b, gi, 0, 0)),
        out_shape=jax.ShapeDtypeStruct((bsz, N_GROUPS) + oshape, _MXU_DTYPE),
        compiler_params=_params(("parallel", "arbitrary"), blocks, temps=6 * _nbytes((nchunk, CMP_HIDDEN), F32)),
        name="compress_k" if is_key else "compress_v",
    )(proj3, pos, w1.astype(_MXU_DTYPE), b1.reshape(1, CMP_HIDDEN), w2.astype(_MXU_DTYPE), b2.reshape(1, HEAD_DIM),
      g.reshape(1, HEAD_DIM), *tables)


def _nsa_kernel(q_ref, kc_ref, vct_ref, ks_ref, vst_ref, kw_ref, vwt_ref, oh_ref, ov_ref, gt_ref, o_ref,
                m_sc, l_sc, acc_sc, *, tq, tk):
    dt = q_ref.dtype
    nc = kc_ref.shape[0]
    q0 = pl.program_id(2) * tq
    tiles_per_tk = tk // KEY_TILE
    pcols = HPG * tq // SLC_PARTS

    q = q_ref[...]
    q_t = jnp.concatenate([q[:, h * HEAD_DIM:(h + 1) * HEAD_DIM].astype(F32).T for h in range(HPG)],
                          axis=1).astype(dt)
    t_q = q0 + lax.broadcasted_iota(jnp.int32, (1, tq), 1)
    t_c = jnp.concatenate([t_q] * HPG, axis=1)

    s = _dot(kc_ref[...], q_t)
    cmp_end = lax.broadcasted_iota(jnp.int32, (nc, 1), 0) * CMP_STRIDE + (CMP_BLOCK - 1)
    m = cmp_end <= t_c
    sm = jnp.where(m, s, NEG_INF)
    e = jnp.where(m, jnp.exp2(sm - jnp.max(sm, axis=0, keepdims=True)), 0.0)
    den = jnp.sum(e, axis=0, keepdims=True)
    p = e * (1.0 / jnp.where(den > 0.0, den, 1.0))
    o_cmp = _dot(vct_ref[...], p.astype(dt))

    wlen = WINDOW + tq
    k0w = pl.multiple_of(jnp.maximum(q0 - WINDOW, 0), KEY_TILE)
    sw = _dot(kw_ref[pl.ds(k0w, wlen), :], q_t)
    kpos_w = k0w + lax.broadcasted_iota(jnp.int32, (wlen, 1), 0)
    mw = (kpos_w <= t_c) & (kpos_w > t_c - WINDOW)
    sw = jnp.where(mw, sw, NEG_INF)
    ew = jnp.exp2(sw - jnp.max(sw, axis=0, keepdims=True))
    kt0 = k0w // KEY_TILE
    vw_t = jnp.concatenate([vwt_ref[kt0 + j] for j in range(wlen // KEY_TILE)], axis=1)
    o_win = _dot(vw_t, ew.astype(dt)) * (1.0 / jnp.sum(ew, axis=0, keepdims=True))

    psum = p[:, 0:tq]
    for h in range(1, HPG):
        psum = psum + p[:, h * tq:(h + 1) * tq]
    p_hi = psum.astype(dt)
    p_lo = (psum - p_hi.astype(F32)).astype(dt)
    ov = ov_ref[...]
    imp = _dot(ov, p_hi) + _dot(ov, p_lo)

    blk = lax.broadcasted_iota(jnp.int32, (SEL_BLOCKS, tq), 0)
    cur = jnp.right_shift(t_q, SLC_SHIFT)
    forced = (blk == 0) | (blk == cur) | (blk == cur - 1)
    valid = blk * SLC_BLOCK <= t_q
    work = jnp.where(forced, POS_BIG, jnp.where(valid, imp, NEG_INF))
    blk_f = blk.astype(F32)
    sel = jnp.zeros((SEL_BLOCKS, tq), jnp.bool_)
    for _ in range(SLC_TOPK):
        top = jnp.max(work, axis=0, keepdims=True)
        first = jnp.min(jnp.where(work == top, blk_f, float(SEL_BLOCKS)), axis=0, keepdims=True)
        pick = blk_f == first
        sel = sel | pick
        work = jnp.where(pick, -jnp.inf, work)
    bias = jnp.where(sel, 0.0, NEG_INF).astype(dt)

    qa_t = jnp.concatenate([q_t, jnp.concatenate([bias] * HPG, axis=1)], axis=0)
    m_sc[...] = jnp.full_like(m_sc, NEG_INF)
    l_sc[...] = jnp.zeros_like(l_sc)
    acc_sc[...] = jnp.zeros_like(acc_sc)

    def scores(kt):
        k0 = pl.multiple_of(kt * tk, tk)
        ka = jnp.concatenate([ks_ref[pl.ds(k0, tk), :], oh_ref[pl.ds(k0, tk), :]], axis=1)
        return _dot(ka, qa_t)

    def accumulate(kt, sc, causal):
        if causal:
            kpos = kt * tk + lax.broadcasted_iota(jnp.int32, (tk, 1), 0)
            sc = jnp.where(kpos <= t_c, sc, NEG_INF)
        m_old = m_sc[...]
        m_new = jnp.maximum(m_old, jnp.max(sc, axis=0, keepdims=True))
        alpha = jnp.exp2(m_old - m_new)
        pr = jnp.exp2(sc - m_new)
        l_sc[...] = alpha * l_sc[...] + jnp.sum(pr, axis=0, keepdims=True)
        v_t = jnp.concatenate([vst_ref[kt * tiles_per_tk + j] for j in range(tiles_per_tk)], axis=1)
        acc_sc[...] = alpha * acc_sc[...] + _dot(v_t, pr.astype(dt))
        m_sc[...] = m_new

    n_full = q0 // tk

    def body(kt, sc):
        nxt = scores(kt + 1)
        accumulate(kt, sc, False)
        return nxt

    sc_last = lax.fori_loop(0, n_full, body, scores(0))
    accumulate(n_full, sc_last, True)
    o_slc = acc_sc[...] * (1.0 / l_sc[...])

    gate_t = jax.nn.sigmoid(gt_ref[...]).T
    for h in range(HPG):
        c = slice(h * tq, (h + 1) * tq)
        o = (gate_t[3 * h:3 * h + 1] * o_cmp[:, c] + gate_t[3 * h + 1:3 * h + 2] * o_slc[:, c]
             + gate_t[3 * h + 2:3 * h + 3] * o_win[:, c])
        o_ref[:, h * HEAD_DIM:(h + 1) * HEAD_DIM] = o.T.astype(o_ref.dtype)


def _nsa(qh, kh, vst, vwt, k_cmp, v_cmp_t, proj3, tq=256, tk=512):
    bsz, seq, _ = qh.shape
    nc = k_cmp.shape[2]
    dt = qh.dtype
    assert seq % tk == 0 and tk % tq == 0 and tq % KEY_TILE == 0 and seq >= WINDOW + tq
    assert SLC_TOPK <= seq // SLC_BLOCK <= SEL_BLOCKS
    gt0 = COL_GT // LANES
    cols = HPG * tq
    qw = HPG * HEAD_DIM
    nkt = seq // KEY_TILE
    key_blk = lax.broadcasted_iota(jnp.int32, (seq, SEL_BLOCKS), 0) // SLC_BLOCK
    onehot = (key_blk == lax.broadcasted_iota(jnp.int32, (seq, SEL_BLOCKS), 1)).astype(dt)
    blk = lax.broadcasted_iota(jnp.int32, (SEL_BLOCKS, nc), 0)
    cmp = lax.broadcasted_iota(jnp.int32, (SEL_BLOCKS, nc), 1)
    ratio = SLC_BLOCK // CMP_STRIDE
    back = (CMP_BLOCK - 1) // CMP_STRIDE
    overlap_t = ((cmp >= ratio * blk - back) & (cmp <= ratio * blk + (ratio - 1))).astype(dt)

    kspec = lambda off: pl.BlockSpec((None, seq, HEAD_DIM), lambda b, g, i: (b, 0, off + g))
    vtspec = pl.BlockSpec((None, None, nkt, HEAD_DIM, KEY_TILE), lambda b, g, i: (b, g, 0, 0, 0))
    blocks = ([((tq, qw), dt)] * 2 + [((nc, HEAD_DIM), dt)] * 2 + [((seq, HEAD_DIM), dt)] * 5
              + [((SEL_BLOCKS, nc), dt), ((tq, LANES), F32)])
    scratch = 2 * _nbytes((SUBLANES, cols), F32) + _nbytes((HEAD_DIM, cols), F32)
    return pl.pallas_call(
        functools.partial(_nsa_kernel, tq=tq, tk=tk),
        grid=(bsz, N_GROUPS, seq // tq),
        in_specs=[pl.BlockSpec((None, tq, qw), lambda b, g, i: (b, i, g)),
                  pl.BlockSpec((None, None, nc, HEAD_DIM), lambda b, g, i: (b, g, 0, 0)),
                  pl.BlockSpec((None, None, HEAD_DIM, nc), lambda b, g, i: (b, g, 0, 0)),
                  kspec(0), vtspec, kspec(N_GROUPS), vtspec,
                  pl.BlockSpec((seq, SEL_BLOCKS), lambda b, g, i: (0, 0)),
                  pl.BlockSpec((SEL_BLOCKS, nc), lambda b, g, i: (0, 0)),
                  pl.BlockSpec((None, tq, LANES), lambda b, g, i: (b, i, gt0 + g))],
        out_specs=pl.BlockSpec((None, tq, qw), lambda b, g, i: (b, i, g)),
        out_shape=jax.ShapeDtypeStruct((bsz, seq, NSA_W), _MXU_DTYPE),
        scratch_shapes=[pltpu.VMEM((1, cols), F32), pltpu.VMEM((1, cols), F32), pltpu.VMEM((HEAD_DIM, cols), F32)],
        compiler_params=_params(("parallel", "parallel", "arbitrary"), blocks, scratch=scratch,
                                temps=8 * _nbytes((WINDOW + tq, cols), F32)),
        name="nsa_attention",
    )(qh, k_cmp, v_cmp_t, kh, vst, kh, vwt, onehot, overlap_t, proj3)


def _lru_kernel(x_ref, y_ref, cw_ref, cb_ref, wg_ref, bg_ref, lam_ref, o_ref, xbuf, a_sc, b_sc, h_sc, hprev, *, ts):
    @pl.when(pl.program_id(1) == 0)
    def _():
        xbuf[0:SUBLANES, :] = jnp.zeros((SUBLANES, LRU_W), F32)
        hprev[...] = jnp.zeros_like(hprev)

    x = x_ref[...]
    xbuf[SUBLANES:SUBLANES + ts, :] = x
    xc = cb_ref[...] + cw_ref[CONV_W - 1:CONV_W, :] * x
    for d in range(1, CONV_W):
        xc = xc + cw_ref[CONV_W - 1 - d:CONV_W - d, :] * xbuf[SUBLANES - d:SUBLANES - d + ts, :]
    xbuf[0:SUBLANES, :] = x[ts - SUBLANES:ts, :]

    z = -lam_ref[...]
    softplus = jnp.maximum(z, 0.0) + jnp.log1p(jnp.exp(-jnp.abs(z)))
    for k in range(LRU_BLOCKS):
        sl = slice(k * LRU_BD, (k + 1) * LRU_BD)
        xk = xc[:, sl]
        gates = _dot(xk.astype(_MXU_DTYPE), wg_ref[k])
        r = jax.nn.sigmoid(gates[:, :LRU_BD] + bg_ref[0:1, sl])
        i = jax.nn.sigmoid(gates[:, LRU_BD:] + bg_ref[1:2, sl])
        log_a = (-LRU_C * r) * softplus[:, sl]
        a = jnp.exp(log_a)
        a_sc[:, sl] = a
        b_sc[:, sl] = jnp.sqrt(-jnp.tanh(log_a) * (a * a + 1.0)) * (i * xk)

    row = lax.broadcasted_iota(jnp.int32, (SUBLANES, LRU_W), 0)

    def chunk(c, h_in):
        r0 = pl.multiple_of(c * SUBLANES, SUBLANES)
        a = a_sc[pl.ds(r0, SUBLANES), :]
        b = b_sc[pl.ds(r0, SUBLANES), :]
        d = 1
        while d < SUBLANES:
            keep = row >= d
            a_prev = jnp.where(keep, pltpu.roll(a, d, 0), 1.0)
            b_prev = jnp.where(keep, pltpu.roll(b, d, 0), 0.0)
            b = a * b_prev + b
            a = a * a_prev
            d *= 2
        h = a * h_in + b
        h_sc[pl.ds(r0, SUBLANES), :] = h
        return jnp.broadcast_to(h[SUBLANES - 1:SUBLANES, :], (SUBLANES, LRU_W))

    hprev[...] = lax.fori_loop(0, ts // SUBLANES, chunk, hprev[...])
    o_ref[...] = (h_sc[...] * jax.nn.gelu(y_ref[...])).astype(o_ref.dtype)


def _lru(proj3, conv_w, conv_b, wg, bg, lam, ts=256):
    bsz, seq, _ = proj3.shape
    const = lambda shape: pl.BlockSpec(shape, lambda b, s: (0,) * len(shape))
    blocks = [((ts, LRU_W), F32)] * 2 + [((LRU_BLOCKS, LRU_BD, 2 * LRU_BD), _MXU_DTYPE), ((ts, LRU_W), _MXU_DTYPE)]
    scratch = 3 * _nbytes((ts, LRU_W), F32) + _nbytes((ts + 2 * SUBLANES, LRU_W), F32)
    return pl.pallas_call(
        functools.partial(_lru_kernel, ts=ts),
        grid=(bsz, seq // ts),
        in_specs=[pl.BlockSpec((None, ts, LRU_W), lambda b, s: (b, s, COL_LX // LRU_W)),
                  pl.BlockSpec((None, ts, LRU_W), lambda b, s: (b, s, COL_LY // LRU_W)),
                  const((CONV_W, LRU_W)), const((1, LRU_W)), const((LRU_BLOCKS, LRU_BD, 2 * LRU_BD)),
                  const((2, LRU_W)), const((1, LRU_W))],
        out_specs=pl.BlockSpec((None, ts, LRU_W), lambda b, s: (b, s, 0)),
        out_shape=jax.ShapeDtypeStruct((bsz, seq, LRU_W), _MXU_DTYPE),
        scratch_shapes=[pltpu.VMEM((ts + SUBLANES, LRU_W), F32), pltpu.VMEM((ts, LRU_W), F32),
                        pltpu.VMEM((ts, LRU_W), F32), pltpu.VMEM((ts, LRU_W), F32),
                        pltpu.VMEM((SUBLANES, LRU_W), F32)],
        compiler_params=_params(("parallel", "arbitrary"), blocks, scratch=scratch,
                                temps=4 * _nbytes((ts, LRU_W), F32)),
        name="rglru",
    )(proj3, proj3, conv_w, conv_b.reshape(1, LRU_W), wg, bg, lam.reshape(1, LRU_W))


def _pack_w_in(w_in):
    o_q, o_kv, o_gt = 0, NSA_W, NSA_W + 6 * KV_W
    o_lx = o_gt + N_HEADS * 3
    o_ly = o_lx + LRU_W
    o_mg = o_ly + LRU_W
    gt = w_in[:, o_gt:o_lx].reshape(D_MODEL, N_GROUPS, N_GATE)
    gt = jnp.pad(gt, ((0, 0), (0, 0), (0, LANES - N_GATE))).reshape(D_MODEL, N_GROUPS * LANES)
    parts = [w_in[:, o_q:o_kv], w_in[:, o_lx:o_ly], w_in[:, o_ly:o_mg], w_in[:, o_kv:o_gt], w_in[:, o_mg:], gt]
    return jnp.concatenate(parts, axis=1).astype(_MXU_DTYPE)


def _layer(x, norm1_g, w_in, q_norm_g, k_norm_g, cmp_pos, cmp_w1, cmp_b1, cmp_w2, cmp_b2, conv_w, conv_b,
           lru_w_gates, lru_b_gates, lru_lambda, w_branch_a, w_branch_b, w_out, norm2_g, w_ffn_in, w_ffn_out):
    bsz, seq, _ = x.shape
    t = bsz * seq
    tm = min(1024, t)
    x2 = x.reshape(t, D_MODEL)

    h = _rmsnorm(x2, norm1_g)
    proj = _in_proj(h, _pack_w_in(w_in), tm, 512)
    proj3 = proj.reshape(bsz, seq, PROJ_W)

    tables = _rope_tables(jnp.arange(seq))
    qh, kh, vst, vwt = _prep(proj, q_norm_g, k_norm_g, tables, bsz, seq)
    cmp_tables = _rope_tables(jnp.arange(seq // CMP_STRIDE) * CMP_STRIDE + (CMP_BLOCK - 1))
    k_cmp = _compress(proj3, 0, cmp_pos[0], cmp_w1[0], cmp_b1[0], cmp_w2[0], cmp_b2[0], k_norm_g[0], cmp_tables, True)
    v_cmp_t = _compress(proj3, 1, cmp_pos[1], cmp_w1[1], cmp_b1[1], cmp_w2[1], cmp_b2[1], k_norm_g[0], cmp_tables,
                        False)
    o_nsa = _nsa(qh.reshape(bsz, seq, NSA_W), kh.reshape(bsz, seq, 2 * KV_W), vst, vwt, k_cmp, v_cmp_t, proj3)

    wg = jnp.concatenate([lru_w_gates[0], lru_w_gates[1]], axis=-1).astype(_MXU_DTYPE)
    o_lru = _lru(proj3, conv_w, conv_b, wg, lru_b_gates, lru_lambda)

    mixed = _merge(o_nsa.reshape(t, NSA_W), o_lru.reshape(t, LRU_W), w_branch_a.astype(_MXU_DTYPE),
                   w_branch_b.astype(_MXU_DTYPE), proj, tm, 512)
    x1 = _out_proj(mixed, w_out.astype(_MXU_DTYPE), x2, tm, 512)

    h2 = _rmsnorm(x1, norm2_g)
    pad = FFN_HP - FFN_H
    zcols = jnp.zeros((D_MODEL, pad), F32)
    wgu = jnp.concatenate([w_ffn_in[:, :FFN_H], zcols, w_ffn_in[:, FFN_H:], zcols], axis=1).astype(_MXU_DTYPE)
    act = _ffn_in(h2, wgu, tm, 512)
    w2 = jnp.concatenate([w_ffn_out, jnp.zeros((pad, D_MODEL), F32)], axis=0).astype(_MXU_DTYPE)
    out = _ffn_out(act, w2, x1, tm, 1024, FFN_HP // 8)
    return out.reshape(bsz, seq, D_MODEL)


def kernel(x, norm1_g, w_in, q_norm_g, k_norm_g, cmp_pos, cmp_w1, cmp_b1, cmp_w2, cmp_b2, conv_w, conv_b,
           lru_w_gates, lru_b_gates, lru_lambda, w_branch_a, w_branch_b, w_out, norm2_g, w_ffn_in, w_ffn_out):
    for l in range(norm1_g.shape[0]):
        x = _layer(x, norm1_g[l], w_in[l], q_norm_g[l], k_norm_g[l], cmp_pos[l], cmp_w1[l], cmp_b1[l], cmp_w2[l],
                   cmp_b2[l], conv_w[l], conv_b[l], lru_w_gates[l], lru_b_gates[l], lru_lambda[l], w_branch_a[l],
                   w_branch_b[l], w_out[l], norm2_g[l], w_ffn_in[l], w_ffn_out[l])
    return x
```
